```python
import math
import jax, jax.numpy as jnp
from jax import lax
import numpy as np

D_MODEL = 2048
BATCH = 4
SEQ = 8192
DEPTH = 1

D_SSM = 1024
SSM_GROUP = 16
N_SSM_GROUPS = D_SSM // SSM_GROUP
SSM_STATE = 64
DT_MIN = 0.001
DT_MAX = 0.1
N_Q_HEADS = 16
N_KV_HEADS = 4
HEAD_DIM = 64
Q_PER_KV = N_Q_HEADS // N_KV_HEADS
D_ATTN = N_Q_HEADS * HEAD_DIM
D_KV = N_KV_HEADS * HEAD_DIM
WINDOW = 128
BLOCK = 128
N_BUCKETS = 32
MAX_DISTANCE = 128
N_BRANCHES = 2
D_IN = D_SSM + D_SSM + D_ATTN + D_KV + D_KV + D_ATTN + N_BRANCHES * D_MODEL
DEEPNORM_ALPHA = (2.0 * DEPTH) ** 0.25
DEEPNORM_BETA = (8.0 * DEPTH) ** -0.25
LN_EPS = 1e-5
NEG_INF = -1e30

kernel_name = "hybrid_s5_swa_sink_gated_deepnorm"


def _split_columns(proj):
    sizes = (D_SSM, D_SSM, D_ATTN, D_KV, D_KV, D_ATTN, N_BRANCHES * D_MODEL)
    points = []
    acc = 0
    for s in sizes[:-1]:
        acc += s
        points.append(acc)
    return jnp.split(proj, points, axis=-1)


def _layer_norm(x, gain, bias):
    xf = x.astype(jnp.float32)
    mu = jnp.mean(xf, axis=-1, keepdims=True)
    var = jnp.mean(jnp.square(xf - mu), axis=-1, keepdims=True)
    y = (xf - mu) * lax.rsqrt(var + LN_EPS) * gain.astype(jnp.float32) + bias.astype(jnp.float32)
    return y.astype(x.dtype)


def _t5_causal_bucket(dist):
    max_exact = N_BUCKETS // 2
    is_small = dist < max_exact
    d = jnp.maximum(dist, 1).astype(jnp.float32)
    large = max_exact + (jnp.log(d / max_exact) / math.log(MAX_DISTANCE / max_exact)
                         * (N_BUCKETS - max_exact)).astype(jnp.int32)
    large = jnp.minimum(large, N_BUCKETS - 1)
    return jnp.where(is_small, dist, large)


def _band_bias_and_mask(rel_bias_table, n_blocks):
    i = jnp.arange(BLOCK)[:, None]
    j = jnp.arange(2 * BLOCK)[None, :]
    dist = BLOCK + i - j
    band_ok = (dist >= 0) & (dist < WINDOW)
    bucket = _t5_causal_bucket(jnp.clip(dist, 0, None))
    bias = rel_bias_table.astype(jnp.float32)[bucket]
    bias = jnp.transpose(bias, (2, 0, 1)).reshape(N_KV_HEADS, Q_PER_KV, BLOCK, 2 * BLOCK)
    n = jnp.arange(n_blocks)[:, None, None]
    key_abs = n * BLOCK - BLOCK + j[None]
    mask = band_ok[None] & (key_abs >= 0)
    return bias, mask[None, :, None, None]


def _sliding_window_gqa(q, k, v, sinks, rel_bias_table):
    b, s, _ = q.shape
    nb = s // BLOCK
    q = q.reshape(b, nb, BLOCK, N_KV_HEADS, Q_PER_KV, HEAD_DIM)

    def band(t):
        t = t.reshape(b, s, N_KV_HEADS, HEAD_DIM)
        t = jnp.pad(t, ((0, 0), (BLOCK, 0), (0, 0), (0, 0))).reshape(b, nb + 1, BLOCK, N_KV_HEADS, HEAD_DIM)
        return jnp.concatenate([t[:, :-1], t[:, 1:]], axis=2)

    kb, vb = band(k), band(v)
    bias, mask = _band_bias_and_mask(rel_bias_table, nb)
    logits = jnp.einsum("bnqkgd,bnskd->bnkgqs", q, kb).astype(jnp.float32) * (HEAD_DIM ** -0.5)
    logits = jnp.where(mask, logits + bias, NEG_INF)
    sink = sinks.astype(jnp.float32).reshape(N_KV_HEADS, Q_PER_KV)[None, None, :, :, None, None]
    m = jnp.maximum(jnp.max(logits, axis=-1, keepdims=True), sink)
    p = jnp.exp(logits - m)
    p = p / (jnp.sum(p, axis=-1, keepdims=True) + jnp.exp(sink - m))
    out = jnp.einsum("bnkgqs,bnskd->bnqkgd", p.astype(vb.dtype), vb)
    return out.reshape(b, s, D_ATTN)


def _s5_scan_op(e1, e2):
    a1, b1 = e1
    a2, b2 = e2
    return a1 * a2, a2 * b1 + b2


def _s5_ssm(u, lam_re, lam_im, b_re, b_im, c_re, c_im, d_skip, log_step):
    b, s, _ = u.shape
    f32 = jnp.float32
    step = jnp.exp(log_step.astype(f32))[:, None]
    lam = lax.complex(lam_re.astype(f32), lam_im.astype(f32))
    lam_bar = jnp.exp(lam * step)
    b_cplx = lax.complex(b_re.astype(f32), b_im.astype(f32))
    b_bar = ((lam_bar - 1.0) / lam)[..., None] * b_cplx
    ug = u.astype(f32).reshape(b, s, N_SSM_GROUPS, SSM_GROUP)
    bu = lax.complex(jnp.einsum("bsgh,gph->sbgp", ug, jnp.real(b_bar)),
                     jnp.einsum("bsgh,gph->sbgp", ug, jnp.imag(b_bar)))
    a = jnp.broadcast_to(lam_bar[None, None], (s, 1, N_SSM_GROUPS, SSM_STATE))
    _, states = lax.associative_scan(_s5_scan_op, (a, bu), axis=0)
    y = (jnp.einsum("sbgp,ghp->bsgh", jnp.real(states), c_re.astype(f32))
         - jnp.einsum("sbgp,ghp->bsgh", jnp.imag(states), c_im.astype(f32)))
    y = y + d_skip.astype(f32).reshape(N_SSM_GROUPS, SSM_GROUP) * ug
    return y.reshape(b, s, D_SSM)


def setup_inputs(seed: int = 0) -> dict:
    key = jax.random.key(seed)
    ks = jax.random.split(key, 20)
    f32 = jnp.float32
    x = jax.random.normal(ks[0], (BATCH, SEQ, D_MODEL), f32)
    w_in = jax.random.normal(ks[1], (DEPTH, D_MODEL, D_IN), f32) * D_MODEL ** -0.5
    n_idx = jnp.arange(SSM_STATE, dtype=f32)
    ssm_lambda_re = -0.5 + 0.01 * jax.random.normal(ks[2], (DEPTH, N_SSM_GROUPS, SSM_STATE), f32)
    ssm_lambda_im = math.pi * n_idx + 0.01 * jax.random.normal(ks[3], (DEPTH, N_SSM_GROUPS, SSM_STATE), f32)
    ssm_b_re = jax.random.normal(ks[4], (DEPTH, N_SSM_GROUPS, SSM_STATE, SSM_GROUP), f32) * (2.0 * SSM_GROUP) ** -0.5
    ssm_b_im = jax.random.normal(ks[5], (DEPTH, N_SSM_GROUPS, SSM_STATE, SSM_GROUP), f32) * (2.0 * SSM_GROUP) ** -0.5
    ssm_c_re = jax.random.normal(ks[6], (DEPTH, N_SSM_GROUPS, SSM_GROUP, SSM_STATE), f32) * SSM_STATE ** -0.5
    ssm_c_im = jax.random.normal(ks[7], (DEPTH, N_SSM_GROUPS, SSM_GROUP, SSM_STATE), f32) * SSM_STATE ** -0.5
    ssm_d = jax.random.normal(ks[8], (DEPTH, D_SSM), f32)
    ssm_log_step = jax.random.uniform(ks[9], (DEPTH, N_SSM_GROUPS), f32,
                                      minval=math.log(DT_MIN), maxval=math.log(DT_MAX))
    w_glu = jax.random.normal(ks[10], (DEPTH, D_SSM, 2 * D_SSM), f32) * D_SSM ** -0.5
    attn_sinks = jax.random.normal(ks[11], (DEPTH, N_Q_HEADS), f32)
    rel_bias_table = 0.5 * jax.random.normal(ks[12], (N_BUCKETS, N_Q_HEADS), f32)
    w_branch_ssm = jax.random.normal(ks[13], (DEPTH, D_SSM, D_MODEL), f32) * D_SSM ** -0.5 * DEEPNORM_BETA
    w_branch_attn = jax.random.normal(ks[14], (DEPTH, D_ATTN, D_MODEL), f32) * D_ATTN ** -0.5 * DEEPNORM_BETA
    w_out = jax.random.normal(ks[15], (DEPTH, D_MODEL, D_MODEL), f32) * D_MODEL ** -0.5 * DEEPNORM_BETA
    ln_gain = 1.0 + 0.02 * jax.random.normal(ks[16], (DEPTH, D_MODEL), f32)
    ln_bias = 0.02 * jax.random.normal(ks[17], (DEPTH, D_MODEL), f32)
    return {"x": x, "w_in": w_in, "ssm_lambda_re": ssm_lambda_re, "ssm_lambda_im": ssm_lambda_im,
            "ssm_b_re": ssm_b_re, "ssm_b_im": ssm_b_im, "ssm_c_re": ssm_c_re, "ssm_c_im": ssm_c_im,
            "ssm_d": ssm_d, "ssm_log_step": ssm_log_step, "w_glu": w_glu, "attn_sinks": attn_sinks,
            "rel_bias_table": rel_bias_table, "w_branch_ssm": w_branch_ssm, "w_branch_attn": w_branch_attn,
            "w_out": w_out, "ln_gain": ln_gain, "ln_bias": ln_bias}


def reference(x, w_in, ssm_lambda_re, ssm_lambda_im, ssm_b_re, ssm_b_im, ssm_c_re, ssm_c_im,
              ssm_d, ssm_log_step, w_glu, attn_sinks, rel_bias_table, w_branch_ssm, w_branch_attn,
              w_out, ln_gain, ln_bias):
    for layer in range(DEPTH):
        proj = jnp.einsum("bsd,de->bse", x, w_in[layer])
        u_ssm, z_ssm, q, k, v, z_attn, gate_logits = _split_columns(proj)

        y_ssm = _s5_ssm(u_ssm, ssm_lambda_re[layer], ssm_lambda_im[layer], ssm_b_re[layer], ssm_b_im[layer],
                        ssm_c_re[layer], ssm_c_im[layer], ssm_d[layer], ssm_log_step[layer])
        glu_in = jax.nn.gelu(y_ssm, approximate=False)
        glu_a, glu_b = jnp.split(jnp.einsum("bsc,ce->bse", glu_in, w_glu[layer].astype(jnp.float32)), 2, axis=-1)
        h_ssm = (glu_a * jax.nn.sigmoid(glu_b)).astype(x.dtype) * jax.nn.silu(z_ssm)

        h_attn = _sliding_window_gqa(q, k, v, attn_sinks[layer], rel_bias_table) * jax.nn.silu(z_attn)

        gates = jax.nn.sigmoid(gate_logits.astype(jnp.float32)).astype(x.dtype)
        gate_ssm, gate_attn = jnp.split(gates, 2, axis=-1)
        merged = (gate_ssm * jnp.einsum("bsc,cd->bsd", h_ssm, w_branch_ssm[layer])
                  + gate_attn * jnp.einsum("bsc,cd->bsd", h_attn, w_branch_attn[layer]))
        out = jnp.einsum("bsd,de->bse", merged, w_out[layer])

        x = _layer_norm(DEEPNORM_ALPHA * x + out.astype(x.dtype), ln_gain[layer], ln_bias[layer])
    return x
```

```python
import functools
import math

import jax
import jax.numpy as jnp
from jax import lax
from jax.experimental import pallas as pl
from jax.experimental.pallas import tpu as pltpu

D_MODEL = 2048
D_SSM = 1024
SSM_GROUP = 16
N_SSM_GROUPS = D_SSM // SSM_GROUP
SSM_STATE = 64
N_Q_HEADS = 16
N_KV_HEADS = 4
HEAD_DIM = 64
Q_PER_KV = N_Q_HEADS // N_KV_HEADS
D_ATTN = N_Q_HEADS * HEAD_DIM
D_KV = N_KV_HEADS * HEAD_DIM
WINDOW = 128
BLOCK = 128
N_BUCKETS = 32
MAX_DISTANCE = 128
N_BRANCHES = 2
D_IN = 2 * D_SSM + 2 * D_ATTN + 2 * D_KV + N_BRANCHES * D_MODEL
DEPTH = 1
DEEPNORM_ALPHA = (2.0 * DEPTH) ** 0.25
LN_EPS = 1e-5
NEG_INF = -1e30

LANES = 128
GROUPS_PER_LANE_BLOCK = LANES // SSM_GROUP
N_LANE_BLOCKS = D_SSM // LANES
SSM_CHUNK = 16
STATE_COLS = GROUPS_PER_LANE_BLOCK * SSM_STATE

COL_U = 0
COL_ZSSM = D_SSM
COL_Q = 2 * D_SSM
COL_ZATTN = COL_Q + D_ATTN
COL_GATES = COL_ZATTN + D_ATTN
COL_KV = COL_GATES + N_BRANCHES * D_MODEL

VMEM_LIMIT = 60 * 1024 * 1024

F32 = jnp.float32
BF16 = jnp.bfloat16


def _mm(a, b):
    return jnp.dot(a, b, preferred_element_type=F32)


def _mm_nt(a, b, precision=None):
    return lax.dot_general(a, b, (((1,), (1,)), ((), ())), preferred_element_type=F32, precision=precision)


def _ssm_prep_kernel(lre_ref, lim_ref, ls_ref, btre_ref, btim_ref, cre_ref, cim_ref,
                     kd_ref, gtre_ref, gtim_ref, w2re_ref, w2im_ref, are_ref, aim_ref):
    L = SSM_CHUNK
    for g in range(GROUPS_PER_LANE_BLOCK):
        lre = lre_ref[0, g:g + 1, :]
        lim = lim_ref[0, g:g + 1, :]
        step = jnp.exp(ls_ref[0, g:g + 1, :])
        d = lax.broadcasted_iota(jnp.int32, (L + 1, SSM_STATE), 0).astype(F32)
        mag = jnp.exp(d * (lre * step))
        ang = d * (lim * step)
        e_re = mag * jnp.cos(ang)
        e_im = mag * jnp.sin(ang)
        n_re = e_re[1:2, :] - 1.0
        n_im = e_im[1:2, :]
        den = lre * lre + lim * lim
        k_re = (n_re * lre + n_im * lim) / den
        k_im = (n_im * lre - n_re * lim) / den
        bt_re = btre_ref[0, g]
        bt_im = btim_ref[0, g]
        bb_re = k_re * bt_re - k_im * bt_im
        bb_im = k_re * bt_im + k_im * bt_re
        c_re = cre_ref[0, g]
        c_im = cim_ref[0, g]
        for i in range(L):
            rows = slice(i * SSM_GROUP, (i + 1) * SSM_GROUP)
            er, ei = e_re[i:i + 1, :], e_im[i:i + 1, :]
            gtre_ref[g, rows, :] = er * bb_re - ei * bb_im
            gtim_ref[g, rows, :] = er * bb_im + ei * bb_re
            er1, ei1 = e_re[i + 1:i + 2, :], e_im[i + 1:i + 2, :]
            w2re_ref[g, rows, :] = c_re * er1 - c_im * ei1
            w2im_ref[g, rows, :] = -(c_re * ei1 + c_im * er1)
        hi = lax.Precision.HIGHEST
        kd_ref[g] = (_mm_nt(gtre_ref[g], c_re, precision=hi) - _mm_nt(gtim_ref[g], c_im, precision=hi))
        are_ref[0, g:g + 1, :] = e_re[L:L + 1, :]
        aim_ref[0, g:g + 1, :] = e_im[L:L + 1, :]


def _ssm_prep(lam_re, lam_im, log_step, b_re, b_im, c_re, c_im):
    nb, gb, P, H, L = N_LANE_BLOCKS, GROUPS_PER_LANE_BLOCK, SSM_STATE, SSM_GROUP, SSM_CHUNK
    lre = lam_re.reshape(nb, gb, P)
    lim = lam_im.reshape(nb, gb, P)
    ls = jnp.broadcast_to(log_step[:, None], (N_SSM_GROUPS, P)).reshape(nb, gb, P)
    bt_re = jnp.swapaxes(b_re, 1, 2).reshape(nb, gb, H, P)
    bt_im = jnp.swapaxes(b_im, 1, 2).reshape(nb, gb, H, P)
    cr = c_re.reshape(nb, gb, H, P)
    ci = c_im.reshape(nb, gb, H, P)
    vec = pl.BlockSpec((1, gb, P), lambda i: (i, 0, 0))
    mat = pl.BlockSpec((1, gb, H, P), lambda i: (i, 0, 0, 0))
    big = pl.BlockSpec((gb, L * H, P), lambda i: (i, 0, 0))
    outs = pl.pallas_call(
        _ssm_prep_kernel,
        grid=(nb,),
        in_specs=[vec, vec, vec, mat, mat, mat, mat],
        out_specs=[pl.BlockSpec((gb, L * H, H), lambda i: (i, 0, 0)), big, big, big, big, vec, vec],
        out_shape=[jax.ShapeDtypeStruct((N_SSM_GROUPS, L * H, H), F32)]
        + [jax.ShapeDtypeStruct((N_SSM_GROUPS, L * H, P), F32)] * 4
        + [jax.ShapeDtypeStruct((nb, gb, P), F32)] * 2,
        name="ssm_prep",
    )(lre, lim, ls, bt_re, bt_im, cr, ci)
    kd, gt_re, gt_im, w2_re, w2_im, a_re, a_im = outs

    eye = jnp.eye(gb, dtype=F32)
    kd6 = kd.reshape(nb, gb, L, H, H)
    kb = kd6[:, :, :, :, None, :] * eye[None, :, None, None, :, None]
    kb = jnp.transpose(kb, (0, 2, 1, 3, 4, 5)).reshape(nb, L, LANES, LANES).astype(BF16)

    gt = jnp.stack([gt_re, gt_im], axis=2).reshape(nb, gb, L, H, 2, P)
    gt = gt[:, :, ::-1]
    kend = gt[..., None, :] * eye[None, :, None, None, None, :, None]
    kend = jnp.transpose(kend, (0, 2, 1, 3, 4, 5, 6)).reshape(nb, L * LANES, 2 * STATE_COLS).astype(BF16)

    w2 = jnp.stack([w2_re, w2_im], axis=2).reshape(nb, gb, L, H, 2, P)
    w2 = w2[..., None] * eye[None, :, None, None, None, None, :]
    w2 = jnp.transpose(w2, (0, 4, 1, 5, 2, 6, 3)).reshape(nb, 2 * STATE_COLS, L * LANES).astype(BF16)

    a = jnp.concatenate([a_re.reshape(nb, 1, STATE_COLS), a_im.reshape(nb, 1, STATE_COLS)], axis=-1)
    return kb, kend, w2, a


def _in_proj_kernel(x_ref, w_ref, o_ref, xb_ref):
    @pl.when(pl.program_id(1) == 0)
    def _():
        xb_ref[...] = x_ref[...].astype(BF16)

    o_ref[...] = _mm(xb_ref[...], w_ref[...]).astype(o_ref.dtype)


def _in_proj(x2, w_perm, tm=512, tn=2176):
    m = x2.shape[0]
    return pl.pallas_call(
        _in_proj_kernel,
        grid=(m // tm, D_IN // tn),
        in_specs=[pl.BlockSpec((tm, D_MODEL), lambda i, j: (i, 0)),
                  pl.BlockSpec((D_MODEL, tn), lambda i, j: (0, j))],
        out_specs=pl.BlockSpec((tm, tn), lambda i, j: (i, j)),
        out_shape=jax.ShapeDtypeStruct((m, D_IN), BF16),
        scratch_shapes=[pltpu.VMEM((tm, D_MODEL), BF16)],
        compiler_params=pltpu.CompilerParams(dimension_semantics=("arbitrary", "arbitrary"),
                                             vmem_limit_bytes=VMEM_LIMIT),
        name="in_proj",
    )(x2, w_perm)


def _gelu(y):
    return 0.5 * y * (1.0 + lax.erf(y * (1.0 / math.sqrt(2.0))))


def _ssm_kernel(u_ref, kb_ref, kend_ref, w2_ref, a_ref, d_ref, o_ref, toep_ref, nat_ref, u2_ref, sloc_ref, sin_ref):
    L = SSM_CHUNK
    seq = u_ref.shape[0]
    nc = seq // L

    @pl.when(pl.program_id(1) == 0)
    def _():
        toep_ref[...] = jnp.zeros_like(toep_ref)
        for tau in range(L):
            for t in range(tau, L):
                toep_ref[tau * LANES:(tau + 1) * LANES, t * LANES:(t + 1) * LANES] = kb_ref[t - tau]

    nat_ref[...] = u_ref[...].astype(F32)
    for tau in range(L):
        u2_ref[:, tau * LANES:(tau + 1) * LANES] = nat_ref[pl.ds(tau, nc, stride=L), :].astype(BF16)
    u2 = u2_ref[...]

    sloc_ref[...] = _mm(u2, kend_ref[...])
    a_re = a_ref[:, :STATE_COLS]
    a_im = a_ref[:, STATE_COLS:]

    def scan_step(c, carry):
        s_re, s_im = carry
        sin_ref[pl.ds(c, 1), :STATE_COLS] = s_re
        sin_ref[pl.ds(c, 1), STATE_COLS:] = s_im
        x_re = sloc_ref[pl.ds(c, 1), :STATE_COLS]
        x_im = sloc_ref[pl.ds(c, 1), STATE_COLS:]
        return (a_re * s_re - a_im * s_im + x_re, a_re * s_im + a_im * s_re + x_im)

    zero = jnp.zeros((1, STATE_COLS), F32)
    lax.fori_loop(0, nc, scan_step, (zero, zero), unroll=8)

    y2 = _mm(u2, toep_ref[...]) + _mm(sin_ref[...].astype(BF16), w2_ref[...])
    for t in range(L):
        nat_ref[pl.ds(t, nc, stride=L), :] = y2[:, t * LANES:(t + 1) * LANES]
    y = nat_ref[...] + d_ref[...] * u_ref[...].astype(F32)
    o_ref[...] = _gelu(y).astype(o_ref.dtype)


def _ssm(proj, kb, kend, w2, a, d_skip, batch, seq):
    L = SSM_CHUNK
    nc = seq // L
    d3 = d_skip.reshape(N_LANE_BLOCKS, 1, LANES)
    return pl.pallas_call(
        _ssm_kernel,
        grid=(N_LANE_BLOCKS, batch),
        in_specs=[pl.BlockSpec((seq, LANES), lambda cb, b: (b, cb)),
                  pl.BlockSpec((None, L, LANES, LANES), lambda cb, b: (cb, 0, 0, 0)),
                  pl.BlockSpec((None, L * LANES, 2 * STATE_COLS), lambda cb, b: (cb, 0, 0)),
                  pl.BlockSpec((None, 2 * STATE_COLS, L * LANES), lambda cb, b: (cb, 0, 0)),
                  pl.BlockSpec((None, 1, 2 * STATE_COLS), lambda cb, b: (cb, 0, 0)),
                  pl.BlockSpec((None, 1, LANES), lambda cb, b: (cb, 0, 0))],
        out_specs=pl.BlockSpec((seq, LANES), lambda cb, b: (b, cb)),
        out_shape=jax.ShapeDtypeStruct((batch * seq, D_SSM), BF16),
        scratch_shapes=[pltpu.VMEM((L * LANES, L * LANES), BF16),
                        pltpu.VMEM((seq, LANES), F32),
                        pltpu.VMEM((nc, L * LANES), BF16),
                        pltpu.VMEM((nc, 2 * STATE_COLS), F32),
                        pltpu.VMEM((nc, 2 * STATE_COLS), F32)],
        compiler_params=pltpu.CompilerParams(dimension_semantics=("arbitrary", "arbitrary"),
                                             vmem_limit_bytes=VMEM_LIMIT),
        name="ssm_chunks",
    )(proj, kb, kend, w2, a, d3)


def _glu_kernel(g_ref, z_ref, w_ref, o_ref):
    r = _mm(g_ref[...], w_ref[...])
    z = z_ref[...].astype(F32)
    h = r[:, :D_SSM] * jax.nn.sigmoid(r[:, D_SSM:]) * (z * jax.nn.sigmoid(z))
    o_ref[...] = h.astype(o_ref.dtype)


def _glu(g, proj, w_glu_b, tm=512):
    m = g.shape[0]
    return pl.pallas_call(
        _glu_kernel,
        grid=(m // tm,),
        in_specs=[pl.BlockSpec((tm, D_SSM), lambda i: (i, 0)),
                  pl.BlockSpec((tm, D_SSM), lambda i: (i, COL_ZSSM // D_SSM)),
                  pl.BlockSpec((D_SSM, 2 * D_SSM), lambda i: (0, 0))],
        out_specs=pl.BlockSpec((tm, D_SSM), lambda i: (i, 0)),
        out_shape=jax.ShapeDtypeStruct((m, D_SSM), BF16),
        compiler_params=pltpu.CompilerParams(dimension_semantics=("arbitrary",), vmem_limit_bytes=VMEM_LIMIT),
        name="glu_gate",
    )(g, proj, w_glu_b)


def _swa_kernel(tab_ref, sink_ref, bucket_ref, q_ref, z_ref, kvc_ref, kvp_ref, o_ref, bias_ref):
    n = pl.program_id(1)
    first = jnp.logical_and(pl.program_id(0) == 0, n == 0)

    @pl.when(first)
    def _():
        i = lax.broadcasted_iota(jnp.int32, (BLOCK, 2 * BLOCK), 0)
        j = lax.broadcasted_iota(jnp.int32, (BLOCK, 2 * BLOCK), 1)
        dist = BLOCK + i - j
        band_ok = jnp.logical_and(dist >= 0, dist < WINDOW)
        bucket = bucket_ref[...]
        for h in range(N_Q_HEADS):
            bias = jnp.zeros((BLOCK, 2 * BLOCK), F32)
            for bk in range(N_BUCKETS):
                bias = jnp.where(bucket == bk, tab_ref[bk, h], bias)
            bias_ref[1, h] = jnp.where(band_ok, bias, NEG_INF)
            bias_ref[0, h] = jnp.where(jnp.logical_and(band_ok, j >= BLOCK), bias, NEG_INF)

    tq = q_ref.shape[0]
    for s in range(tq // BLOCK):
        if s == 0:
            kv = jnp.concatenate([kvp_ref[...], kvc_ref[0:BLOCK, :]], axis=0)
            variant = jnp.where(n == 0, 0, 1)
        else:
            kv = kvc_ref[(s - 1) * BLOCK:(s + 1) * BLOCK, :]
            variant = 1
        rows = slice(s * BLOCK, (s + 1) * BLOCK)
        outs = []
        for k in range(N_KV_HEADS):
            kb = kv[:, k * HEAD_DIM:(k + 1) * HEAD_DIM]
            vb = kv[:, D_KV + k * HEAD_DIM:D_KV + (k + 1) * HEAD_DIM]
            for g in range(Q_PER_KV):
                h = k * Q_PER_KV + g
                qh = q_ref[rows, h * HEAD_DIM:(h + 1) * HEAD_DIM]
                logits = _mm_nt(qh, kb) + bias_ref[variant, h]
                sink = sink_ref[h]
                m = jnp.maximum(jnp.max(logits, axis=-1, keepdims=True), sink)
                p = jnp.exp(logits - m)
                denom = jnp.sum(p, axis=-1, keepdims=True) + jnp.exp(sink - m)
                outs.append(_mm(p.astype(BF16), vb) / denom)
        z = z_ref[rows, :].astype(F32)
        o_ref[rows, :] = (jnp.concatenate(outs, axis=-1) * (z * jax.nn.sigmoid(z))).astype(o_ref.dtype)


def _t5_bucket_band():
    i = jnp.arange(BLOCK)[:, None]
    j = jnp.arange(2 * BLOCK)[None, :]
    dist = jnp.clip(BLOCK + i - j, 0, None)
    max_exact = N_BUCKETS // 2
    d = jnp.maximum(dist, 1).astype(F32)
    large = max_exact + (jnp.log(d / max_exact) / math.log(MAX_DISTANCE / max_exact)
                         * (N_BUCKETS - max_exact)).astype(jnp.int32)
    large = jnp.minimum(large, N_BUCKETS - 1)
    return jnp.where(dist < max_exact, dist, large).astype(jnp.int32)


def _swa(proj, sinks, rel_bias_table, batch, seq, tq=512):
    nq = seq // tq
    blocks_per_tile = tq // BLOCK
    smem = pl.BlockSpec(memory_space=pltpu.SMEM)
    return pl.pallas_call(
        _swa_kernel,
        grid=(batch, nq),
        in_specs=[smem, smem,
                  pl.BlockSpec((BLOCK, 2 * BLOCK), lambda b, n: (0, 0)),
                  pl.BlockSpec((tq, D_ATTN), lambda b, n: (b * nq + n, COL_Q // D_ATTN)),
                  pl.BlockSpec((tq, D_ATTN), lambda b, n: (b * nq + n, COL_ZATTN // D_ATTN)),
                  pl.BlockSpec((tq, 2 * D_KV), lambda b, n: (b * nq + n, COL_KV // (2 * D_KV))),
                  pl.BlockSpec((BLOCK, 2 * D_KV),
                               lambda b, n: (jnp.maximum((b * nq + n) * blocks_per_tile - 1, 0),
                                             COL_KV // (2 * D_KV)))],
        out_specs=pl.BlockSpec((tq, D_ATTN), lambda b, n: (b * nq + n, 0)),
        out_shape=jax.ShapeDtypeStruct((batch * seq, D_ATTN), BF16),
        scratch_shapes=[pltpu.VMEM((2, N_Q_HEADS, BLOCK, 2 * BLOCK), F32)],
        compiler_params=pltpu.CompilerParams(dimension_semantics=("arbitrary", "arbitrary"),
                                             vmem_limit_bytes=VMEM_LIMIT),
        name="swa",
    )(rel_bias_table, sinks, _t5_bucket_band(), proj, proj, proj, proj)


def _merge_kernel(hs_ref, ha_ref, gt_ref, x_ref, wbs_ref, wba_ref, wo_ref, gain_ref, bias_ref, o_ref):
    gates = jax.nn.sigmoid(gt_ref[...].astype(F32))
    merged = (gates[:, :D_MODEL] * _mm(hs_ref[...], wbs_ref[...])
              + gates[:, D_MODEL:] * _mm(ha_ref[...], wba_ref[...]))
    r = DEEPNORM_ALPHA * x_ref[...] + _mm(merged.astype(BF16), wo_ref[...])
    mu = jnp.mean(r, axis=-1, keepdims=True)
    c = r - mu
    var = jnp.mean(c * c, axis=-1, keepdims=True)
    o_ref[...] = c * lax.rsqrt(var + LN_EPS) * gain_ref[...] + bias_ref[...]


def _merge(h_ssm, h_attn, proj, x2, wbs, wba, wo, gain, bias, tm=256):
    m = x2.shape[0]
    const = lambda i: (0, 0)
    return pl.pallas_call(
        _merge_kernel,
        grid=(m // tm,),
        in_specs=[pl.BlockSpec((tm, D_SSM), lambda i: (i, 0)),
                  pl.BlockSpec((tm, D_ATTN), lambda i: (i, 0)),
                  pl.BlockSpec((tm, N_BRANCHES * D_MODEL), lambda i: (i, COL_GATES // (N_BRANCHES * D_MODEL))),
                  pl.BlockSpec((tm, D_MODEL), lambda i: (i, 0)),
                  pl.BlockSpec((D_SSM, D_MODEL), const),
                  pl.BlockSpec((D_ATTN, D_MODEL), const),
                  pl.BlockSpec((D_MODEL, D_MODEL), const),
                  pl.BlockSpec((1, D_MODEL), const),
                  pl.BlockSpec((1, D_MODEL), const)],
        out_specs=pl.BlockSpec((tm, D_MODEL), lambda i: (i, 0)),
        out_shape=jax.ShapeDtypeStruct((m, D_MODEL), F32),
        compiler_params=pltpu.CompilerParams(dimension_semantics=("arbitrary",), vmem_limit_bytes=VMEM_LIMIT),
        name="merge_out",
    )(h_ssm, h_attn, proj, x2, wbs, wba, wo, gain, bias)


def kernel(x, w_in, ssm_lambda_re, ssm_lambda_im, ssm_b_re, ssm_b_im, ssm_c_re, ssm_c_im, ssm_d, ssm_log_step,
           w_glu, attn_sinks, rel_bias_table, w_branch_ssm, w_branch_attn, w_out, ln_gain, ln_bias):
    batch, seq, _ = x.shape
    for layer in range(w_in.shape[0]):
        x2 = x.reshape(batch * seq, D_MODEL)
        w = w_in[layer]
        k0 = COL_Q + D_ATTN
        w_perm = jnp.concatenate(
            [w[:, :COL_Q], w[:, COL_Q:k0] * (HEAD_DIM ** -0.5), w[:, k0 + 2 * D_KV:], w[:, k0:k0 + 2 * D_KV]],
            axis=1).astype(BF16)
        kb, kend, w2, a = _ssm_prep(ssm_lambda_re[layer], ssm_lambda_im[layer], ssm_log_step[layer],
                                    ssm_b_re[layer], ssm_b_im[layer], ssm_c_re[layer], ssm_c_im[layer])
        proj = _in_proj(x2, w_perm)
        g = _ssm(proj, kb, kend, w2, a, ssm_d[layer], batch, seq)
        h_ssm = _glu(g, proj, w_glu[layer].astype(BF16))
        h_attn = _swa(proj, attn_sinks[layer], rel_bias_table, batch, seq)
        out = _merge(h_ssm, h_attn, proj, x2, w_branch_ssm[layer].astype(BF16), w_branch_attn[layer].astype(BF16),
                     w_out[layer].astype(BF16), ln_gain[layer].reshape(1, D_MODEL), ln_bias[layer].reshape(1, D_MODEL))
        x = out.reshape(batch, seq, D_MODEL)
    return x
```

```python
import functools
import math

import jax
import jax.numpy as jnp
from jax import lax
from jax.experimental import pallas as pl
from jax.experimental.pallas import tpu as pltpu

D_MODEL = 2048
D_SSM = 1024
SSM_GROUP = 16
N_SSM_GROUPS = D_SSM // SSM_GROUP
SSM_STATE = 64
N_Q_HEADS = 16
N_KV_HEADS = 4
HEAD_DIM = 64
Q_PER_KV = N_Q_HEADS // N_KV_HEADS
D_ATTN = N_Q_HEADS * HEAD_DIM
D_KV = N_KV_HEADS * HEAD_DIM
WINDOW = 128
BLOCK = 128
N_BUCKETS = 32
MAX_DISTANCE = 128
N_BRANCHES = 2
D_IN = 2 * D_SSM + 2 * D_ATTN + 2 * D_KV + N_BRANCHES * D_MODEL
DEPTH = 1
DEEPNORM_ALPHA = (2.0 * DEPTH) ** 0.25
LN_EPS = 1e-5
NEG_INF = -1e30

LANES = 128
GROUPS_PER_LANE_BLOCK = LANES // SSM_GROUP
N_LANE_BLOCKS = D_SSM // LANES
SSM_CHUNK = 16
STATE_COLS = GROUPS_PER_LANE_BLOCK * SSM_STATE

COL_U = 0
COL_ZSSM = D_SSM
COL_Q = 2 * D_SSM
COL_ZATTN = COL_Q + D_ATTN
COL_GATES = COL_ZATTN + D_ATTN
COL_KV = COL_GATES + N_BRANCHES * D_MODEL

VMEM_LIMIT = 60 * 1024 * 1024

F32 = jnp.float32
BF16 = jnp.bfloat16


def _mm(a, b):
    return jnp.dot(a, b, preferred_element_type=F32)


def _mm_nt(a, b, precision=None):
    return lax.dot_general(a, b, (((1,), (1,)), ((), ())), preferred_element_type=F32, precision=precision)


def _ssm_prep_kernel(lre_ref, lim_ref, ls_ref, btre_ref, btim_ref, cre_ref, cim_ref,
                     kb_ref, kend_ref, w2t_ref, a_ref, gtre_ref, gtim_ref, cbre_ref, cbim_ref):
    L, H, gb, sc = SSM_CHUNK, SSM_GROUP, GROUPS_PER_LANE_BLOCK, STATE_COLS
    lre = lre_ref[...]
    lim = lim_ref[...]
    step = jnp.exp(ls_ref[...])
    d = lax.broadcasted_iota(jnp.int32, (L + 1, sc), 0).astype(F32)
    mag = jnp.exp(d * (lre * step))
    ang = d * (lim * step)
    e_re = mag * jnp.cos(ang)
    e_im = mag * jnp.sin(ang)
    n_re = e_re[1:2, :] - 1.0
    n_im = e_im[1:2, :]
    den = lre * lre + lim * lim
    k_re = (n_re * lre + n_im * lim) / den
    k_im = (n_im * lre - n_re * lim) / den
    bt_re, bt_im = btre_ref[...], btim_ref[...]
    bb_re = k_re * bt_re - k_im * bt_im
    bb_im = k_re * bt_im + k_im * bt_re
    c_re, c_im = cre_ref[...], cim_ref[...]
    state_group = lax.broadcasted_iota(jnp.int32, (H, sc), 1) // SSM_STATE
    masks = [state_group == g for g in range(gb)]

    def put(ref, r0, re, im):
        for g in range(gb):
            rows = slice(r0 + g * H, r0 + (g + 1) * H)
            ref[rows, :sc] = jnp.where(masks[g], re, 0.0).astype(ref.dtype)
            ref[rows, sc:] = jnp.where(masks[g], im, 0.0).astype(ref.dtype)

    for i in range(L):
        er, ei = e_re[i:i + 1, :], e_im[i:i + 1, :]
        g_re = er * bb_re - ei * bb_im
        g_im = er * bb_im + ei * bb_re
        gtre_ref[i * H:(i + 1) * H, :] = g_re
        gtim_ref[i * H:(i + 1) * H, :] = g_im
        put(kend_ref, (L - 1 - i) * LANES, g_re, g_im)
        er1, ei1 = e_re[i + 1:i + 2, :], e_im[i + 1:i + 2, :]
        put(w2t_ref, i * LANES, c_re * er1 - c_im * ei1, -(c_re * ei1 + c_im * er1))
    a_ref[:, :sc] = e_re[L:L + 1, :]
    a_ref[:, sc:] = e_im[L:L + 1, :]

    for g in range(gb):
        cbre_ref[g * H:(g + 1) * H, :] = jnp.where(masks[g], c_re, 0.0)
        cbim_ref[g * H:(g + 1) * H, :] = jnp.where(masks[g], c_im, 0.0)
    hi = lax.Precision.HIGHEST
    kd = _mm_nt(gtre_ref[...], cbre_ref[...], precision=hi) - _mm_nt(gtim_ref[...], cbim_ref[...], precision=hi)
    out_group = lax.broadcasted_iota(jnp.int32, (H, LANES), 1) // H
    for dlag in range(L):
        blk = kd[dlag * H:(dlag + 1) * H, :]
        for g in range(gb):
            kb_ref[dlag, g * H:(g + 1) * H, :] = jnp.where(out_group == g, blk, 0.0).astype(kb_ref.dtype)


def _ssm_prep(lam_re, lam_im, log_step, b_re, b_im, c_re, c_im):
    nb, gb, P, H, L, sc = N_LANE_BLOCKS, GROUPS_PER_LANE_BLOCK, SSM_STATE, SSM_GROUP, SSM_CHUNK, STATE_COLS
    lre = lam_re.reshape(nb, 1, sc)
    lim = lam_im.reshape(nb, 1, sc)
    ls = jnp.broadcast_to(log_step[:, None], (N_SSM_GROUPS, P)).reshape(nb, 1, sc)
    bt_re = jnp.transpose(b_re.reshape(nb, gb, P, H), (0, 3, 1, 2)).reshape(nb, H, sc)
    bt_im = jnp.transpose(b_im.reshape(nb, gb, P, H), (0, 3, 1, 2)).reshape(nb, H, sc)
    cr = jnp.transpose(c_re.reshape(nb, gb, H, P), (0, 2, 1, 3)).reshape(nb, H, sc)
    ci = jnp.transpose(c_im.reshape(nb, gb, H, P), (0, 2, 1, 3)).reshape(nb, H, sc)
    vec = pl.BlockSpec((None, 1, sc), lambda i: (i, 0, 0))
    mat = pl.BlockSpec((None, H, sc), lambda i: (i, 0, 0))
    op = pl.BlockSpec((None, L * LANES, 2 * sc), lambda i: (i, 0, 0))
    return pl.pallas_call(
        _ssm_prep_kernel,
        grid=(nb,),
        in_specs=[vec, vec, vec, mat, mat, mat, mat],
        out_specs=[pl.BlockSpec((None, L, LANES, LANES), lambda i: (i, 0, 0, 0)), op, op,
                   pl.BlockSpec((None, 1, 2 * sc), lambda i: (i, 0, 0))],
        out_shape=[jax.ShapeDtypeStruct((nb, L, LANES, LANES), BF16),
                   jax.ShapeDtypeStruct((nb, L * LANES, 2 * sc), BF16),
                   jax.ShapeDtypeStruct((nb, L * LANES, 2 * sc), BF16),
                   jax.ShapeDtypeStruct((nb, 1, 2 * sc), F32)],
        scratch_shapes=[pltpu.VMEM((L * H, sc), F32), pltpu.VMEM((L * H, sc), F32),
                        pltpu.VMEM((LANES, sc), F32), pltpu.VMEM((LANES, sc), F32)],
        compiler_params=pltpu.CompilerParams(dimension_semantics=("arbitrary",), vmem_limit_bytes=VMEM_LIMIT),
        name="ssm_prep",
    )(lre, lim, ls, bt_re, bt_im, cr, ci)


def _in_proj_kernel(x_ref, w_ref, o_ref, xb_ref):
    @pl.when(pl.program_id(1) == 0)
    def _():
        xb_ref[...] = x_ref[...].astype(BF16)

    o_ref[...] = _mm(xb_ref[...], w_ref[...]).astype(o_ref.dtype)


def _in_proj(x2, w_perm, tm=512, tn=2176):
    m = x2.shape[0]
    return pl.pallas_call(
        _in_proj_kernel,
        grid=(m // tm, D_IN // tn),
        in_specs=[pl.BlockSpec((tm, D_MODEL), lambda i, j: (i, 0)),
                  pl.BlockSpec((D_MODEL, tn), lambda i, j: (0, j))],
        out_specs=pl.BlockSpec((tm, tn), lambda i, j: (i, j)),
        out_shape=jax.ShapeDtypeStruct((m, D_IN), BF16),
        scratch_shapes=[pltpu.VMEM((tm, D_MODEL), BF16)],
        compiler_params=pltpu.CompilerParams(dimension_semantics=("arbitrary", "arbitrary"),
                                             vmem_limit_bytes=VMEM_LIMIT),
        name="in_proj",
    )(x2, w_perm)


def _gelu(y):
    return 0.5 * y * (1.0 + lax.erf(y * (1.0 / math.sqrt(2.0))))


def _ssm_kernel(u_ref, kb_ref, kend_ref, w2t_ref, a_ref, d_ref, o_ref, toep_ref, nat_ref, u2_ref, sloc_ref, sin_ref):
    L = SSM_CHUNK
    seq = u_ref.shape[0]
    nc = seq // L

    @pl.when(pl.program_id(1) == 0)
    def _():
        toep_ref[...] = jnp.zeros_like(toep_ref)
        for tau in range(L):
            for t in range(tau, L):
                toep_ref[tau * LANES:(tau + 1) * LANES, t * LANES:(t + 1) * LANES] = kb_ref[t - tau]

    nat_ref[...] = u_ref[...].astype(F32)
    for tau in range(L):
        u2_ref[:, tau * LANES:(tau + 1) * LANES] = nat_ref[pl.ds(tau, nc, stride=L), :].astype(BF16)
    u2 = u2_ref[...]

    sloc_ref[...] = _mm(u2, kend_ref[...])
    a_re = a_ref[:, :STATE_COLS]
    a_im = a_ref[:, STATE_COLS:]

    def scan_step(c, carry):
        s_re, s_im = carry
        sin_ref[pl.ds(c, 1), :STATE_COLS] = s_re
        sin_ref[pl.ds(c, 1), STATE_COLS:] = s_im
        x_re = sloc_ref[pl.ds(c, 1), :STATE_COLS]
        x_im = sloc_ref[pl.ds(c, 1), STATE_COLS:]
        return (a_re * s_re - a_im * s_im + x_re, a_re * s_im + a_im * s_re + x_im)

    zero = jnp.zeros((1, STATE_COLS), F32)
    lax.fori_loop(0, nc, scan_step, (zero, zero), unroll=8)

    y2 = _mm(u2, toep_ref[...]) + _mm_nt(sin_ref[...].astype(BF16), w2t_ref[...])
    for t in range(L):
        nat_ref[pl.ds(t, nc, stride=L), :] = y2[:, t * LANES:(t + 1) * LANES]
    y = nat_ref[...] + d_ref[...] * u_ref[...].astype(F32)
    o_ref[...] = _gelu(y).astype(o_ref.dtype)


def _ssm(proj, kb, kend, w2t, a, d_skip, batch, seq):
    L = SSM_CHUNK
    nc = seq // L
    d3 = d_skip.reshape(N_LANE_BLOCKS, 1, LANES)
    return pl.pallas_call(
        _ssm_kernel,
        grid=(N_LANE_BLOCKS, batch),
        in_specs=[pl.BlockSpec((seq, LANES), lambda cb, b: (b, cb)),
                  pl.BlockSpec((None, L, LANES, LANES), lambda cb, b: (cb, 0, 0, 0)),
                  pl.BlockSpec((None, L * LANES, 2 * STATE_COLS), lambda cb, b: (cb, 0, 0)),
                  pl.BlockSpec((None, L * LANES, 2 * STATE_COLS), lambda cb, b: (cb, 0, 0)),
                  pl.BlockSpec((None, 1, 2 * STATE_COLS), lambda cb, b: (cb, 0, 0)),
                  pl.BlockSpec((None, 1, LANES), lambda cb, b: (cb, 0, 0))],
        out_specs=pl.BlockSpec((seq, LANES), lambda cb, b: (b, cb)),
        out_shape=jax.ShapeDtypeStruct((batch * seq, D_SSM), BF16),
        scratch_shapes=[pltpu.VMEM((L * LANES, L * LANES), BF16),
                        pltpu.VMEM((seq, LANES), F32),
                        pltpu.VMEM((nc, L * LANES), BF16),
                        pltpu.VMEM((nc, 2 * STATE_COLS), F32),
                        pltpu.VMEM((nc, 2 * STATE_COLS), F32)],
        compiler_params=pltpu.CompilerParams(dimension_semantics=("arbitrary", "arbitrary"),
                                             vmem_limit_bytes=VMEM_LIMIT),
        name="ssm_chunks",
    )(proj, kb, kend, w2t, a, d3)


def _glu_kernel(g_ref, z_ref, w_ref, o_ref):
    r = _mm(g_ref[...], w_ref[...])
    z = z_ref[...].astype(F32)
    h = r[:, :D_SSM] * jax.nn.sigmoid(r[:, D_SSM:]) * (z * jax.nn.sigmoid(z))
    o_ref[...] = h.astype(o_ref.dtype)


def _glu(g, proj, w_glu_b, tm=512):
    m = g.shape[0]
    return pl.pallas_call(
        _glu_kernel,
        grid=(m // tm,),
        in_specs=[pl.BlockSpec((tm, D_SSM), lambda i: (i, 0)),
                  pl.BlockSpec((tm, D_SSM), lambda i: (i, COL_ZSSM // D_SSM)),
                  pl.BlockSpec((D_SSM, 2 * D_SSM), lambda i: (0, 0))],
        out_specs=pl.BlockSpec((tm, D_SSM), lambda i: (i, 0)),
        out_shape=jax.ShapeDtypeStruct((m, D_SSM), BF16),
        compiler_params=pltpu.CompilerParams(dimension_semantics=("arbitrary",), vmem_limit_bytes=VMEM_LIMIT),
        name="glu_gate",
    )(g, proj, w_glu_b)


def _swa_kernel(tab_ref, sink_ref, bucket_ref, q_ref, z_ref, kvc_ref, kvp_ref, o_ref, bias_ref):
    n = pl.program_id(1)
    first = jnp.logical_and(pl.program_id(0) == 0, n == 0)

    @pl.when(first)
    def _():
        i = lax.broadcasted_iota(jnp.int32, (BLOCK, 2 * BLOCK), 0)
        j = lax.broadcasted_iota(jnp.int32, (BLOCK, 2 * BLOCK), 1)
        dist = BLOCK + i - j
        band_ok = jnp.logical_and(dist >= 0, dist < WINDOW)
        bucket = bucket_ref[...]
        for h in range(N_Q_HEADS):
            bias = jnp.zeros((BLOCK, 2 * BLOCK), F32)
            for bk in range(N_BUCKETS):
                bias = jnp.where(bucket == bk, tab_ref[bk, h], bias)
            bias_ref[1, h] = jnp.where(band_ok, bias, NEG_INF)
            bias_ref[0, h] = jnp.where(jnp.logical_and(band_ok, j >= BLOCK), bias, NEG_INF)

    tq = q_ref.shape[0]
    for s in range(tq // BLOCK):
        if s == 0:
            kv = jnp.concatenate([kvp_ref[...], kvc_ref[0:BLOCK, :]], axis=0)
            variant = jnp.where(n == 0, 0, 1)
        else:
            kv = kvc_ref[(s - 1) * BLOCK:(s + 1) * BLOCK, :]
            variant = 1
        rows = slice(s * BLOCK, (s + 1) * BLOCK)
        outs = []
        for k in range(N_KV_HEADS):
            kb = kv[:, k * HEAD_DIM:(k + 1) * HEAD_DIM]
            vb = kv[:, D_KV + k * HEAD_DIM:D_KV + (k + 1) * HEAD_DIM]
            for g in range(Q_PER_KV):
                h = k * Q_PER_KV + g
                qh = q_ref[rows, h * HEAD_DIM:(h + 1) * HEAD_DIM]
                logits = _mm_nt(qh, kb) + bias_ref[variant, h]
                sink = sink_ref[h]
                m = jnp.maximum(jnp.max(logits, axis=-1, keepdims=True), sink)
                p = jnp.exp(logits - m)
                denom = jnp.sum(p, axis=-1, keepdims=True) + jnp.exp(sink - m)
                outs.append(_mm(p.astype(BF16), vb) / denom)
        z = z_ref[rows, :].astype(F32)
        o_ref[rows, :] = (jnp.concatenate(outs, axis=-1) * (z * jax.nn.sigmoid(z))).astype(o_ref.dtype)


def _t5_bucket_band():
    i = jnp.arange(BLOCK)[:, None]
    j = jnp.arange(2 * BLOCK)[None, :]
    dist = jnp.clip(BLOCK + i - j, 0, None)
    max_exact = N_BUCKETS // 2
    d = jnp.maximum(dist, 1).astype(F32)
    large = max_exact + (jnp.log(d / max_exact) / math.log(MAX_DISTANCE / max_exact)
                         * (N_BUCKETS - max_exact)).astype(jnp.int32)
    large = jnp.minimum(large, N_BUCKETS - 1)
    return jnp.where(dist < max_exact, dist, large).astype(jnp.int32)


def _swa(proj, sinks, rel_bias_table, batch, seq, tq=512):
    nq = seq // tq
    blocks_per_tile = tq // BLOCK
    smem = pl.BlockSpec(memory_space=pltpu.SMEM)
    return pl.pallas_call(
        _swa_kernel,
        grid=(batch, nq),
        in_specs=[smem, smem,
                  pl.BlockSpec((BLOCK, 2 * BLOCK), lambda b, n: (0, 0)),
                  pl.BlockSpec((tq, D_ATTN), lambda b, n: (b * nq + n, COL_Q // D_ATTN)),
                  pl.BlockSpec((tq, D_ATTN), lambda b, n: (b * nq + n, COL_ZATTN // D_ATTN)),
                  pl.BlockSpec((tq, 2 * D_KV), lambda b, n: (b * nq + n, COL_KV // (2 * D_KV))),
                  pl.BlockSpec((BLOCK, 2 * D_KV),
                               lambda b, n: (jnp.maximum((b * nq + n) * blocks_per_tile - 1, 0),
                                             COL_KV // (2 * D_KV)))],
        out_specs=pl.BlockSpec((tq, D_ATTN), lambda b, n: (b * nq + n, 0)),
        out_shape=jax.ShapeDtypeStruct((batch * seq, D_ATTN), BF16),
        scratch_shapes=[pltpu.VMEM((2, N_Q_HEADS, BLOCK, 2 * BLOCK), F32)],
        compiler_params=pltpu.CompilerParams(dimension_semantics=("arbitrary", "arbitrary"),
                                             vmem_limit_bytes=VMEM_LIMIT),
        name="swa",
    )(rel_bias_table, sinks, _t5_bucket_band(), proj, proj, proj, proj)


def _merge_kernel(hs_ref, ha_ref, gt_ref, x_ref, wbs_ref, wba_ref, wo_ref, gain_ref, bias_ref, o_ref):
    gates = jax.nn.sigmoid(gt_ref[...].astype(F32))
    merged = (gates[:, :D_MODEL] * _mm(hs_ref[...], wbs_ref[...])
              + gates[:, D_MODEL:] * _mm(ha_ref[...], wba_ref[...]))
    r = DEEPNORM_ALPHA * x_ref[...] + _mm(merged.astype(BF16), wo_ref[...])
    mu = jnp.mean(r, axis=-1, keepdims=True)
    c = r - mu
    var = jnp.mean(c * c, axis=-1, keepdims=True)
    o_ref[...] = c * lax.rsqrt(var + LN_EPS) * gain_ref[...] + bias_ref[...]


def _merge(h_ssm, h_attn, proj, x2, wbs, wba, wo, gain, bias, tm=256):
    m = x2.shape[0]
    const = lambda i: (0, 0)
    return pl.pallas_call(
        _merge_kernel,
        grid=(m // tm,),
        in_specs=[pl.BlockSpec((tm, D_SSM), lambda i: (i, 0)),
                  pl.BlockSpec((tm, D_ATTN), lambda i: (i, 0)),
                  pl.BlockSpec((tm, N_BRANCHES * D_MODEL), lambda i: (i, COL_GATES // (N_BRANCHES * D_MODEL))),
                  pl.BlockSpec((tm, D_MODEL), lambda i: (i, 0)),
                  pl.BlockSpec((D_SSM, D_MODEL), const),
                  pl.BlockSpec((D_ATTN, D_MODEL), const),
                  pl.BlockSpec((D_MODEL, D_MODEL), const),
                  pl.BlockSpec((1, D_MODEL), const),
                  pl.BlockSpec((1, D_MODEL), const)],
        out_specs=pl.BlockSpec((tm, D_MODEL), lambda i: (i, 0)),
        out_shape=jax.ShapeDtypeStruct((m, D_MODEL), F32),
        compiler_params=pltpu.CompilerParams(dimension_semantics=("arbitrary",), vmem_limit_bytes=VMEM_LIMIT),
        name="merge_out",
    )(h_ssm, h_attn, proj, x2, wbs, wba, wo, gain, bias)


def kernel(x, w_in, ssm_lambda_re, ssm_lambda_im, ssm_b_re, ssm_b_im, ssm_c_re, ssm_c_im, ssm_d, ssm_log_step,
           w_glu, attn_sinks, rel_bias_table, w_branch_ssm, w_branch_attn, w_out, ln_gain, ln_bias):
    batch, seq, _ = x.shape
    for layer in range(w_in.shape[0]):
        x2 = x.reshape(batch * seq, D_MODEL)
        w = w_in[layer]
        k0 = COL_Q + D_ATTN
        w_perm = jnp.concatenate(
            [w[:, :COL_Q], w[:, COL_Q:k0] * (HEAD_DIM ** -0.5), w[:, k0 + 2 * D_KV:], w[:, k0:k0 + 2 * D_KV]],
            axis=1).astype(BF16)
        kb, kend, w2t, a = _ssm_prep(ssm_lambda_re[layer], ssm_lambda_im[layer], ssm_log_step[layer],
                                    ssm_b_re[layer], ssm_b_im[layer], ssm_c_re[layer], ssm_c_im[layer])
        proj = _in_proj(x2, w_perm)
        g = _ssm(proj, kb, kend, w2t, a, ssm_d[layer], batch, seq)
        h_ssm = _glu(g, proj, w_glu[layer].astype(BF16))
        h_attn = _swa(proj, attn_sinks[layer], rel_bias_table, batch, seq)
        out = _merge(h_ssm, h_attn, proj, x2, w_branch_ssm[layer].astype(BF16), w_branch_attn[layer].astype(BF16),
                     w_out[layer].astype(BF16), ln_gain[layer].reshape(1, D_MODEL), ln_bias[layer].reshape(1, D_MODEL))
        x = out.reshape(batch, seq, D_MODEL)
    return x
```

```python
import functools
import math

import jax
import jax.numpy as jnp
from jax import lax
from jax.experimental import pallas as pl
from jax.experimental.pallas import tpu as pltpu

D_MODEL = 2048
D_SSM = 1024
SSM_GROUP = 16
N_SSM_GROUPS = D_SSM // SSM_GROUP
SSM_STATE = 64
N_Q_HEADS = 16
N_KV_HEADS = 4
HEAD_DIM = 64
Q_PER_KV = N_Q_HEADS // N_KV_HEADS
D_ATTN = N_Q_HEADS * HEAD_DIM
D_KV = N_KV_HEADS * HEAD_DIM
WINDOW = 128
BLOCK = 128
N_BUCKETS = 32
MAX_DISTANCE = 128
N_BRANCHES = 2
D_IN = 2 * D_SSM + 2 * D_ATTN + 2 * D_KV + N_BRANCHES * D_MODEL
DEPTH = 1
DEEPNORM_ALPHA = (2.0 * DEPTH) ** 0.25
LN_EPS = 1e-5
NEG_INF = -1e30

LANES = 128
GROUPS_PER_LANE_BLOCK = LANES // SSM_GROUP
N_LANE_BLOCKS = D_SSM // LANES
SSM_CHUNK = 16
STATE_COLS = GROUPS_PER_LANE_BLOCK * SSM_STATE

COL_U = 0
COL_ZSSM = D_SSM
COL_Q = 2 * D_SSM
COL_ZATTN = COL_Q + D_ATTN
COL_GATES = COL_ZATTN + D_ATTN
COL_KV = COL_GATES + N_BRANCHES * D_MODEL

VMEM_LIMIT = 60 * 1024 * 1024

F32 = jnp.float32
BF16 = jnp.bfloat16


def _mm(a, b):
    return jnp.dot(a, b, preferred_element_type=F32)


def _mm_nt(a, b, precision=None):
    return lax.dot_general(a, b, (((1,), (1,)), ((), ())), preferred_element_type=F32, precision=precision)


def _ssm_prep_kernel(lre_ref, lim_ref, ls_ref, btre_ref, btim_ref, cre_ref, cim_ref,
                     kb_ref, kend_ref, w2t_ref, a_ref, gtre_ref, gtim_ref, cbre_ref, cbim_ref):
    L, H, gb, sc = SSM_CHUNK, SSM_GROUP, GROUPS_PER_LANE_BLOCK, STATE_COLS
    lre = lre_ref[...]
    lim = lim_ref[...]
    step = jnp.exp(ls_ref[...])
    d = lax.broadcasted_iota(jnp.int32, (L + 1, sc), 0).astype(F32)
    mag = jnp.exp(d * (lre * step))
    ang = d * (lim * step)
    e_re = mag * jnp.cos(ang)
    e_im = mag * jnp.sin(ang)
    n_re = e_re[1:2, :] - 1.0
    n_im = e_im[1:2, :]
    den = lre * lre + lim * lim
    k_re = (n_re * lre + n_im * lim) / den
    k_im = (n_im * lre - n_re * lim) / den
    bt_re, bt_im = btre_ref[...], btim_ref[...]
    bb_re = k_re * bt_re - k_im * bt_im
    bb_im = k_re * bt_im + k_im * bt_re
    c_re, c_im = cre_ref[...], cim_ref[...]
    state_group = lax.broadcasted_iota(jnp.int32, (H, sc), 1) // SSM_STATE
    masks = [state_group == g for g in range(gb)]

    def put(ref, r0, re, im):
        for g in range(gb):
            rows = slice(r0 + g * H, r0 + (g + 1) * H)
            ref[rows, :sc] = jnp.where(masks[g], re, 0.0).astype(ref.dtype)
            ref[rows, sc:] = jnp.where(masks[g], im, 0.0).astype(ref.dtype)

    for i in range(L):
        er, ei = e_re[i:i + 1, :], e_im[i:i + 1, :]
        g_re = er * bb_re - ei * bb_im
        g_im = er * bb_im + ei * bb_re
        gtre_ref[i * H:(i + 1) * H, :] = g_re
        gtim_ref[i * H:(i + 1) * H, :] = g_im
        put(kend_ref, (L - 1 - i) * LANES, g_re, g_im)
        er1, ei1 = e_re[i + 1:i + 2, :], e_im[i + 1:i + 2, :]
        put(w2t_ref, i * LANES, c_re * er1 - c_im * ei1, -(c_re * ei1 + c_im * er1))
    a_ref[:, :sc] = e_re[L:L + 1, :]
    a_ref[:, sc:] = e_im[L:L + 1, :]

    for g in range(gb):
        cbre_ref[g * H:(g + 1) * H, :] = jnp.where(masks[g], c_re, 0.0)
        cbim_ref[g * H:(g + 1) * H, :] = jnp.where(masks[g], c_im, 0.0)
    hi = lax.Precision.HIGHEST
    kd = _mm_nt(gtre_ref[...], cbre_ref[...], precision=hi) - _mm_nt(gtim_ref[...], cbim_ref[...], precision=hi)
    out_group = lax.broadcasted_iota(jnp.int32, (H, LANES), 1) // H
    for dlag in range(L):
        blk = kd[dlag * H:(dlag + 1) * H, :]
        for g in range(gb):
            kb_ref[dlag, g * H:(g + 1) * H, :] = jnp.where(out_group == g, blk, 0.0).astype(kb_ref.dtype)


def _ssm_prep(lam_re, lam_im, log_step, b_re, b_im, c_re, c_im):
    nb, gb, P, H, L, sc = N_LANE_BLOCKS, GROUPS_PER_LANE_BLOCK, SSM_STATE, SSM_GROUP, SSM_CHUNK, STATE_COLS
    lre = lam_re.reshape(nb, 1, sc)
    lim = lam_im.reshape(nb, 1, sc)
    ls = jnp.broadcast_to(log_step[:, None], (N_SSM_GROUPS, P)).reshape(nb, 1, sc)
    bt_re = jnp.transpose(b_re.reshape(nb, gb, P, H), (0, 3, 1, 2)).reshape(nb, H, sc)
    bt_im = jnp.transpose(b_im.reshape(nb, gb, P, H), (0, 3, 1, 2)).reshape(nb, H, sc)
    cr = jnp.transpose(c_re.reshape(nb, gb, H, P), (0, 2, 1, 3)).reshape(nb, H, sc)
    ci = jnp.transpose(c_im.reshape(nb, gb, H, P), (0, 2, 1, 3)).reshape(nb, H, sc)
    vec = pl.BlockSpec((None, 1, sc), lambda i: (i, 0, 0))
    mat = pl.BlockSpec((None, H, sc), lambda i: (i, 0, 0))
    op = pl.BlockSpec((None, L * LANES, 2 * sc), lambda i: (i, 0, 0))
    return pl.pallas_call(
        _ssm_prep_kernel,
        grid=(nb,),
        in_specs=[vec, vec, vec, mat, mat, mat, mat],
        out_specs=[pl.BlockSpec((None, L, LANES, LANES), lambda i: (i, 0, 0, 0)), op, op,
                   pl.BlockSpec((None, 1, 2 * sc), lambda i: (i, 0, 0))],
        out_shape=[jax.ShapeDtypeStruct((nb, L, LANES, LANES), BF16),
                   jax.ShapeDtypeStruct((nb, L * LANES, 2 * sc), BF16),
                   jax.ShapeDtypeStruct((nb, L * LANES, 2 * sc), BF16),
                   jax.ShapeDtypeStruct((nb, 1, 2 * sc), F32)],
        scratch_shapes=[pltpu.VMEM((L * H, sc), F32), pltpu.VMEM((L * H, sc), F32),
                        pltpu.VMEM((LANES, sc), F32), pltpu.VMEM((LANES, sc), F32)],
        compiler_params=pltpu.CompilerParams(dimension_semantics=("arbitrary",), vmem_limit_bytes=VMEM_LIMIT),
        name="ssm_prep",
    )(lre, lim, ls, bt_re, bt_im, cr, ci)


def _in_proj_kernel(x_ref, w_ref, o_ref, xb_ref):
    @pl.when(pl.program_id(1) == 0)
    def _():
        xb_ref[...] = x_ref[...].astype(BF16)

    o_ref[...] = _mm(xb_ref[...], w_ref[...]).astype(o_ref.dtype)


def _in_proj(x2, w_perm, tm=512, tn=2176):
    m = x2.shape[0]
    return pl.pallas_call(
        _in_proj_kernel,
        grid=(m // tm, D_IN // tn),
        in_specs=[pl.BlockSpec((tm, D_MODEL), lambda i, j: (i, 0)),
                  pl.BlockSpec((D_MODEL, tn), lambda i, j: (0, j))],
        out_specs=pl.BlockSpec((tm, tn), lambda i, j: (i, j)),
        out_shape=jax.ShapeDtypeStruct((m, D_IN), BF16),
        scratch_shapes=[pltpu.VMEM((tm, D_MODEL), BF16)],
        compiler_params=pltpu.CompilerParams(dimension_semantics=("arbitrary", "arbitrary"),
                                             vmem_limit_bytes=VMEM_LIMIT),
        name="in_proj",
    )(x2, w_perm)


def _gelu(y):
    return 0.5 * y * (1.0 + lax.erf(y * (1.0 / math.sqrt(2.0))))


def _ssm_kernel(u_ref, kb_ref, kend_ref, w2t_ref, a_ref, d_ref, o_ref, toep_ref, nat_ref, u2_ref, sloc_ref, sin_ref):
    L = SSM_CHUNK
    seq = u_ref.shape[0]
    nc = seq // L

    @pl.when(pl.program_id(1) == 0)
    def _():
        toep_ref[...] = jnp.zeros_like(toep_ref)
        for tau in range(L):
            for t in range(tau, L):
                toep_ref[tau * LANES:(tau + 1) * LANES, t * LANES:(t + 1) * LANES] = kb_ref[t - tau]

    nat_ref[...] = u_ref[...].astype(F32)
    for tau in range(L):
        u2_ref[:, tau * LANES:(tau + 1) * LANES] = nat_ref[pl.ds(tau, nc, stride=L), :].astype(BF16)
    u2 = u2_ref[...]

    sloc_ref[...] = _mm(u2, kend_ref[...])
    a_re = a_ref[:, :STATE_COLS]
    a_im = a_ref[:, STATE_COLS:]

    def scan_step(c, carry):
        s_re, s_im = carry
        sin_ref[pl.ds(c, 1), :STATE_COLS] = s_re
        sin_ref[pl.ds(c, 1), STATE_COLS:] = s_im
        x_re = sloc_ref[pl.ds(c, 1), :STATE_COLS]
        x_im = sloc_ref[pl.ds(c, 1), STATE_COLS:]
        return (a_re * s_re - a_im * s_im + x_re, a_re * s_im + a_im * s_re + x_im)

    zero = jnp.zeros((1, STATE_COLS), F32)
    lax.fori_loop(0, nc, scan_step, (zero, zero), unroll=8)

    y2 = _mm(u2, toep_ref[...]) + _mm_nt(sin_ref[...].astype(BF16), w2t_ref[...])
    for t in range(L):
        nat_ref[pl.ds(t, nc, stride=L), :] = y2[:, t * LANES:(t + 1) * LANES]
    y = nat_ref[...] + d_ref[...] * u_ref[...].astype(F32)
    o_ref[...] = _gelu(y).astype(o_ref.dtype)


def _ssm(proj, kb, kend, w2t, a, d_skip, batch, seq):
    L = SSM_CHUNK
    nc = seq // L
    d3 = d_skip.reshape(N_LANE_BLOCKS, 1, LANES)
    return pl.pallas_call(
        _ssm_kernel,
        grid=(N_LANE_BLOCKS, batch),
        in_specs=[pl.BlockSpec((seq, LANES), lambda cb, b: (b, cb)),
                  pl.BlockSpec((None, L, LANES, LANES), lambda cb, b: (cb, 0, 0, 0)),
                  pl.BlockSpec((None, L * LANES, 2 * STATE_COLS), lambda cb, b: (cb, 0, 0)),
                  pl.BlockSpec((None, L * LANES, 2 * STATE_COLS), lambda cb, b: (cb, 0, 0)),
                  pl.BlockSpec((None, 1, 2 * STATE_COLS), lambda cb, b: (cb, 0, 0)),
                  pl.BlockSpec((None, 1, LANES), lambda cb, b: (cb, 0, 0))],
        out_specs=pl.BlockSpec((seq, LANES), lambda cb, b: (b, cb)),
        out_shape=jax.ShapeDtypeStruct((batch * seq, D_SSM), BF16),
        scratch_shapes=[pltpu.VMEM((L * LANES, L * LANES), BF16),
                        pltpu.VMEM((seq, LANES), F32),
                        pltpu.VMEM((nc, L * LANES), BF16),
                        pltpu.VMEM((nc, 2 * STATE_COLS), F32),
                        pltpu.VMEM((nc, 2 * STATE_COLS), F32)],
        compiler_params=pltpu.CompilerParams(dimension_semantics=("arbitrary", "arbitrary"),
                                             vmem_limit_bytes=VMEM_LIMIT),
        name="ssm_chunks",
    )(proj, kb, kend, w2t, a, d3)


def _glu_kernel(g_ref, z_ref, w_ref, o_ref):
    r = _mm(g_ref[...], w_ref[...])
    z = z_ref[...].astype(F32)
    h = r[:, :D_SSM] * jax.nn.sigmoid(r[:, D_SSM:]) * (z * jax.nn.sigmoid(z))
    o_ref[...] = h.astype(o_ref.dtype)


def _glu(g, proj, w_glu_b, tm=512):
    m = g.shape[0]
    return pl.pallas_call(
        _glu_kernel,
        grid=(m // tm,),
        in_specs=[pl.BlockSpec((tm, D_SSM), lambda i: (i, 0)),
                  pl.BlockSpec((tm, D_SSM), lambda i: (i, COL_ZSSM // D_SSM)),
                  pl.BlockSpec((D_SSM, 2 * D_SSM), lambda i: (0, 0))],
        out_specs=pl.BlockSpec((tm, D_SSM), lambda i: (i, 0)),
        out_shape=jax.ShapeDtypeStruct((m, D_SSM), BF16),
        compiler_params=pltpu.CompilerParams(dimension_semantics=("arbitrary",), vmem_limit_bytes=VMEM_LIMIT),
        name="glu_gate",
    )(g, proj, w_glu_b)


def _swa_kernel(tab_ref, sink_ref, bucket_ref, q_ref, z_ref, kvc_ref, kvp_ref, o_ref, bias_ref):
    n = pl.program_id(1)
    first = jnp.logical_and(pl.program_id(0) == 0, n == 0)
    n_pairs = N_KV_HEADS // 2

    @pl.when(first)
    def _():
        i = lax.broadcasted_iota(jnp.int32, (BLOCK, 2 * BLOCK), 0)
        j = lax.broadcasted_iota(jnp.int32, (BLOCK, 2 * BLOCK), 1)
        dist = BLOCK + i - j
        band_ok = jnp.logical_and(dist >= 0, dist < WINDOW)
        bucket = bucket_ref[...]
        for pair in range(n_pairs):
            for g in range(Q_PER_KV):
                for half in range(2):
                    h = (2 * pair + half) * Q_PER_KV + g
                    bias = jnp.zeros((BLOCK, 2 * BLOCK), F32)
                    for bk in range(N_BUCKETS):
                        bias = jnp.where(bucket == bk, tab_ref[bk, h], bias)
                    rr = slice(g * BLOCK, (g + 1) * BLOCK)
                    cc = slice(half * 2 * BLOCK, (half + 1) * 2 * BLOCK)
                    bias_ref[1, pair, rr, cc] = jnp.where(band_ok, bias, NEG_INF)
                    bias_ref[0, pair, rr, cc] = jnp.where(jnp.logical_and(band_ok, j >= BLOCK), bias, NEG_INF)

    band = 2 * BLOCK
    low_kv = lax.broadcasted_iota(jnp.int32, (band, LANES), 1).astype(F32).astype(BF16) < HEAD_DIM
    low_q = lax.broadcasted_iota(jnp.int32, (BLOCK, LANES), 1) < HEAD_DIM
    zeros_kv = jnp.zeros((band, LANES), BF16)
    ones_kv = jnp.ones((band, LANES), BF16)
    ones_bd = jnp.concatenate([jnp.where(low_kv, ones_kv, zeros_kv), jnp.where(low_kv, zeros_kv, ones_kv)], axis=0)

    tq = q_ref.shape[0]
    for s in range(tq // BLOCK):
        if s == 0:
            kv = jnp.concatenate([kvp_ref[...], kvc_ref[0:BLOCK, :]], axis=0)
            variant = jnp.where(n == 0, 0, 1)
        else:
            kv = kvc_ref[(s - 1) * BLOCK:(s + 1) * BLOCK, :]
            variant = 1
        rows = slice(s * BLOCK, (s + 1) * BLOCK)
        for pair in range(n_pairs):
            tk = kv[:, pair * LANES:(pair + 1) * LANES]
            tv = kv[:, D_KV + pair * LANES:D_KV + (pair + 1) * LANES]
            k_bd = jnp.concatenate([jnp.where(low_kv, tk, zeros_kv), jnp.where(low_kv, zeros_kv, tk)], axis=0)
            v_bd = jnp.concatenate([jnp.where(low_kv, tv, zeros_kv), jnp.where(low_kv, zeros_kv, tv)], axis=0)
            v_aug = jnp.concatenate([v_bd, ones_bd], axis=1)
            tiles = [slice((pair * Q_PER_KV + g) * LANES, (pair * Q_PER_KV + g + 1) * LANES) for g in range(Q_PER_KV)]
            q4 = jnp.concatenate([q_ref[rows, t] for t in tiles], axis=0)
            logits = _mm_nt(q4, k_bd) + bias_ref[variant, pair]
            p_rows, sink_rows = [], []
            for g in range(Q_PER_KV):
                p_half, e_half = [], []
                for half in range(2):
                    sink = sink_ref[(2 * pair + half) * Q_PER_KV + g]
                    lg = logits[g * BLOCK:(g + 1) * BLOCK, half * band:(half + 1) * band]
                    m = jnp.maximum(jnp.max(lg, axis=-1, keepdims=True), sink)
                    p_half.append(jnp.exp(lg - m).astype(BF16))
                    e_half.append(jnp.exp(sink - m))
                p_rows.append(jnp.concatenate(p_half, axis=1))
                sink_rows.append(jnp.where(low_q, e_half[0], e_half[1]))
            oa = _mm(jnp.concatenate(p_rows, axis=0), v_aug)
            o = oa[:, :LANES] / (oa[:, LANES:] + jnp.concatenate(sink_rows, axis=0))
            for g, t in enumerate(tiles):
                z = z_ref[rows, t].astype(F32)
                o_ref[rows, t] = (o[g * BLOCK:(g + 1) * BLOCK, :] * (z * jax.nn.sigmoid(z))).astype(o_ref.dtype)


def _t5_bucket_band():
    i = jnp.arange(BLOCK)[:, None]
    j = jnp.arange(2 * BLOCK)[None, :]
    dist = jnp.clip(BLOCK + i - j, 0, None)
    max_exact = N_BUCKETS // 2
    d = jnp.maximum(dist, 1).astype(F32)
    large = max_exact + (jnp.log(d / max_exact) / math.log(MAX_DISTANCE / max_exact)
                         * (N_BUCKETS - max_exact)).astype(jnp.int32)
    large = jnp.minimum(large, N_BUCKETS - 1)
    return jnp.where(dist < max_exact, dist, large).astype(jnp.int32)


def _swa(proj, sinks, rel_bias_table, batch, seq, tq=512):
    nq = seq // tq
    blocks_per_tile = tq // BLOCK
    smem = pl.BlockSpec(memory_space=pltpu.SMEM)
    return pl.pallas_call(
        _swa_kernel,
        grid=(batch, nq),
        in_specs=[smem, smem,
                  pl.BlockSpec((BLOCK, 2 * BLOCK), lambda b, n: (0, 0)),
                  pl.BlockSpec((tq, D_ATTN), lambda b, n: (b * nq + n, COL_Q // D_ATTN)),
                  pl.BlockSpec((tq, D_ATTN), lambda b, n: (b * nq + n, COL_ZATTN // D_ATTN)),
                  pl.BlockSpec((tq, 2 * D_KV), lambda b, n: (b * nq + n, COL_KV // (2 * D_KV))),
                  pl.BlockSpec((BLOCK, 2 * D_KV),
                               lambda b, n: (jnp.maximum((b * nq + n) * blocks_per_tile - 1, 0),
                                             COL_KV // (2 * D_KV)))],
        out_specs=pl.BlockSpec((tq, D_ATTN), lambda b, n: (b * nq + n, 0)),
        out_shape=jax.ShapeDtypeStruct((batch * seq, D_ATTN), BF16),
        scratch_shapes=[pltpu.VMEM((2, N_KV_HEADS // 2, Q_PER_KV * BLOCK, 4 * BLOCK), F32)],
        compiler_params=pltpu.CompilerParams(dimension_semantics=("arbitrary", "arbitrary"),
                                             vmem_limit_bytes=VMEM_LIMIT),
        name="swa",
    )(rel_bias_table, sinks, _t5_bucket_band(), proj, proj, proj, proj)


def _merge_kernel(hs_ref, ha_ref, gt_ref, x_ref, wbs_ref, wba_ref, wo_ref, gain_ref, bias_ref, o_ref):
    gates = jax.nn.sigmoid(gt_ref[...].astype(F32))
    merged = (gates[:, :D_MODEL] * _mm(hs_ref[...], wbs_ref[...])
              + gates[:, D_MODEL:] * _mm(ha_ref[...], wba_ref[...]))
    r = DEEPNORM_ALPHA * x_ref[...] + _mm(merged.astype(BF16), wo_ref[...])
    mu = jnp.mean(r, axis=-1, keepdims=True)
    c = r - mu
    var = jnp.mean(c * c, axis=-1, keepdims=True)
    o_ref[...] = c * lax.rsqrt(var + LN_EPS) * gain_ref[...] + bias_ref[...]


def _pair_heads(w, axis):
    shape = w.shape
    w = w.reshape(shape[:axis] + (N_KV_HEADS // 2, 2, Q_PER_KV, HEAD_DIM) + shape[axis + 1:])
    w = jnp.swapaxes(w, axis + 1, axis + 2)
    return w.reshape(shape)


def _merge(h_ssm, h_attn, proj, x2, wbs, wba, wo, gain, bias, tm=256):
    m = x2.shape[0]
    const = lambda i: (0, 0)
    return pl.pallas_call(
        _merge_kernel,
        grid=(m // tm,),
        in_specs=[pl.BlockSpec((tm, D_SSM), lambda i: (i, 0)),
                  pl.BlockSpec((tm, D_ATTN), lambda i: (i, 0)),
                  pl.BlockSpec((tm, N_BRANCHES * D_MODEL), lambda i: (i, COL_GATES // (N_BRANCHES * D_MODEL))),
                  pl.BlockSpec((tm, D_MODEL), lambda i: (i, 0)),
                  pl.BlockSpec((D_SSM, D_MODEL), const),
                  pl.BlockSpec((D_ATTN, D_MODEL), const),
                  pl.BlockSpec((D_MODEL, D_MODEL), const),
                  pl.BlockSpec((1, D_MODEL), const),
                  pl.BlockSpec((1, D_MODEL), const)],
        out_specs=pl.BlockSpec((tm, D_MODEL), lambda i: (i, 0)),
        out_shape=jax.ShapeDtypeStruct((m, D_MODEL), F32),
        compiler_params=pltpu.CompilerParams(dimension_semantics=("arbitrary",), vmem_limit_bytes=VMEM_LIMIT),
        name="merge_out",
    )(h_ssm, h_attn, proj, x2, wbs, wba, wo, gain, bias)


def kernel(x, w_in, ssm_lambda_re, ssm_lambda_im, ssm_b_re, ssm_b_im, ssm_c_re, ssm_c_im, ssm_d, ssm_log_step,
           w_glu, attn_sinks, rel_bias_table, w_branch_ssm, w_branch_attn, w_out, ln_gain, ln_bias):
    batch, seq, _ = x.shape
    for layer in range(w_in.shape[0]):
        x2 = x.reshape(batch * seq, D_MODEL)
        w = w_in[layer]
        k0 = COL_Q + D_ATTN
        w_perm = jnp.concatenate(
            [w[:, :COL_Q], _pair_heads(w[:, COL_Q:k0] * (HEAD_DIM ** -0.5), 1),
             _pair_heads(w[:, k0 + 2 * D_KV:k0 + 2 * D_KV + D_ATTN], 1), w[:, k0 + 2 * D_KV + D_ATTN:],
             w[:, k0:k0 + 2 * D_KV]], axis=1).astype(BF16)
        kb, kend, w2t, a = _ssm_prep(ssm_lambda_re[layer], ssm_lambda_im[layer], ssm_log_step[layer],
                                    ssm_b_re[layer], ssm_b_im[layer], ssm_c_re[layer], ssm_c_im[layer])
        proj = _in_proj(x2, w_perm)
        g = _ssm(proj, kb, kend, w2t, a, ssm_d[layer], batch, seq)
        h_ssm = _glu(g, proj, w_glu[layer].astype(BF16))
        h_attn = _swa(proj, attn_sinks[layer], rel_bias_table, batch, seq)
        out = _merge(h_ssm, h_attn, proj, x2, w_branch_ssm[layer].astype(BF16), _pair_heads(w_branch_attn[layer], 0).astype(BF16),
                     w_out[layer].astype(BF16), ln_gain[layer].reshape(1, D_MODEL), ln_bias[layer].reshape(1, D_MODEL))
        x = out.reshape(batch, seq, D_MODEL)
    return x
```
